```python
import jax, jax.numpy as jnp
from jax import lax
import numpy as np

D_MODEL = 1024
BATCH = 8
SEQ = 4096
DEPTH = 1

PLE_DIM = 256
D_FF = 2816
CONV_CH = D_MODEL
CONV_WIDTH = 31
GLA_HEADS = 4
GLA_DK = D_MODEL // 2
GLA_DV = D_MODEL
GLA_HEAD_K = GLA_DK // GLA_HEADS
GLA_HEAD_V = GLA_DV // GLA_HEADS
GLA_GATE_RANK = 16
GLA_TAU = 16.0
GLA_CHUNK = 64
EPS = 1e-6

kernel_name = "hybrid_conformer_gla_parallel_block"


def rms_norm(x, g):
    x32 = x.astype(jnp.float32)
    y = x32 * lax.rsqrt(jnp.mean(x32 * x32, axis=-1, keepdims=True) + EPS)
    return (y * g.astype(jnp.float32)).astype(x.dtype)


def layer_norm(x, g, b):
    x32 = x.astype(jnp.float32)
    mu = jnp.mean(x32, axis=-1, keepdims=True)
    xc = x32 - mu
    y = xc * lax.rsqrt(jnp.mean(xc * xc, axis=-1, keepdims=True) + EPS)
    return (y * g.astype(jnp.float32) + b.astype(jnp.float32)).astype(x.dtype)


def swiglu(x, w_in, w_out):
    gate, up = jnp.split(x @ w_in, 2, axis=-1)
    return (jax.nn.silu(gate) * up) @ w_out


def conformer_conv(a, b, w_dw, b_dw, ln_g, ln_b, w_pw):
    y = a * jax.nn.sigmoid(b)
    y = lax.conv_general_dilated(
        y, w_dw[:, None, :].astype(y.dtype), window_strides=(1,),
        padding=[(CONV_WIDTH - 1, 0)],
        dimension_numbers=("NWC", "WIO", "NWC"),
        feature_group_count=CONV_CH) + b_dw
    y = jax.nn.silu(layer_norm(y, ln_g, ln_b))
    return y @ w_pw


def gla_chunked(q, k, v, log_a):
    B, T, H, dk = q.shape
    dv = v.shape[-1]
    C = GLA_CHUNK
    N = T // C
    q = q.astype(jnp.float32).reshape(B, N, C, H, dk)
    k = k.astype(jnp.float32).reshape(B, N, C, H, dk)
    v = v.astype(jnp.float32).reshape(B, N, C, H, dv)
    cum = jnp.cumsum(log_a.astype(jnp.float32).reshape(B, N, C, H, dk), axis=2)
    cum_last = cum[:, :, -1:]
    q_dec = q * jnp.exp(cum)
    k_inv = k * jnp.exp(-cum)
    k_dec = k * jnp.exp(cum_last - cum)
    scores = jnp.einsum("bnihd,bnjhd->bnhij", q_dec, k_inv)
    causal = jnp.tril(jnp.ones((C, C), dtype=bool))
    scores = jnp.where(causal, scores, 0.0)
    o_intra = jnp.einsum("bnhij,bnjhe->bnihe", scores, v)
    def step(S, xs):
        qd, kd, vv, a_last = xs
        o = jnp.einsum("bchd,bhde->bche", qd, S)
        S = a_last[..., None] * S + jnp.einsum("bchd,bche->bhde", kd, vv)
        return S, o
    xs = (jnp.moveaxis(q_dec, 1, 0), jnp.moveaxis(k_dec, 1, 0), jnp.moveaxis(v, 1, 0),
          jnp.moveaxis(jnp.exp(cum_last[:, :, 0]), 1, 0))
    S0 = jnp.zeros((B, H, dk, dv), jnp.float32)
    _, o_inter = lax.scan(step, S0, xs)
    o = o_intra + jnp.moveaxis(o_inter, 0, 1)
    return o.reshape(B, T, H, dv)


def setup_inputs(seed: int = 0) -> dict:
    key = jax.random.key(seed)
    ks = jax.random.split(key, 32)
    L, D = DEPTH, D_MODEL
    n_in = 2 * CONV_CH + 2 * GLA_DK + 2 * GLA_DV + GLA_GATE_RANK + 2 * D

    def w(k, shape, fan_in):
        return jax.random.normal(k, shape, jnp.float32) * (fan_in ** -0.5)

    def gain(k, shape):
        return 1.0 + 0.05 * jax.random.normal(k, shape, jnp.float32)

    def bias(k, shape, s=0.02):
        return s * jax.random.normal(k, shape, jnp.float32)

    return {
        "x": jax.random.normal(ks[0], (BATCH, SEQ, D), jnp.float32),
        "p": jax.random.normal(ks[1], (DEPTH, BATCH, SEQ, PLE_DIM), jnp.float32),
        "ffn1_norm": gain(ks[2], (L, D)),
        "ffn1_w_in": w(ks[3], (L, D, 2 * D_FF), D),
        "ffn1_w_out": w(ks[4], (L, D_FF, D), D_FF),
        "mix_norm": gain(ks[5], (L, D)),
        "w_mix_in": w(ks[6], (L, D, n_in), D),
        "conv_dw_w": w(ks[7], (L, CONV_WIDTH, CONV_CH), CONV_WIDTH),
        "conv_dw_b": bias(ks[8], (L, CONV_CH)),
        "conv_ln_g": gain(ks[9], (L, CONV_CH)),
        "conv_ln_b": bias(ks[10], (L, CONV_CH)),
        "conv_w_pw": w(ks[11], (L, CONV_CH, D), CONV_CH),
        "gla_w_alpha": w(ks[12], (L, GLA_GATE_RANK, GLA_DK), GLA_GATE_RANK),
        "gla_b_alpha": bias(ks[13], (L, GLA_DK), 0.1),
        "gla_norm": gain(ks[14], (L, GLA_DV)),
        "gla_w_o": w(ks[15], (L, GLA_DV, D), GLA_DV),
        "w_mix_out": w(ks[16], (L, D, D), D),
        "ffn2_norm": gain(ks[17], (L, D)),
        "ffn2_w_in": w(ks[18], (L, D, 2 * D_FF), D),
        "ffn2_w_out": w(ks[19], (L, D_FF, D), D_FF),
        "ple_norm": gain(ks[20], (L, D)),
        "ple_w_gate": w(ks[21], (L, D, D), D),
        "ple_w_proj": w(ks[22], (L, PLE_DIM, D), PLE_DIM),
        "ple_post_norm": gain(ks[23], (L, D)),
        "final_norm": gain(ks[24], (D,)),
    }


def reference(x, p, ffn1_norm, ffn1_w_in, ffn1_w_out, mix_norm, w_mix_in,
              conv_dw_w, conv_dw_b, conv_ln_g, conv_ln_b, conv_w_pw,
              gla_w_alpha, gla_b_alpha, gla_norm, gla_w_o, w_mix_out,
              ffn2_norm, ffn2_w_in, ffn2_w_out,
              ple_norm, ple_w_gate, ple_w_proj, ple_post_norm, final_norm):
    B, T, D = x.shape
    splits = np.cumsum([CONV_CH, CONV_CH, GLA_DK, GLA_DK, GLA_DV, GLA_DV,
                        GLA_GATE_RANK, D_MODEL]).tolist()
    h = x
    for i in range(DEPTH):
        h = h + 0.5 * swiglu(rms_norm(h, ffn1_norm[i]), ffn1_w_in[i], ffn1_w_out[i])

        u = rms_norm(h, mix_norm[i])
        z = u @ w_mix_in[i]
        (z_ca, z_cb, z_q, z_k, z_v, z_g, z_lr, z_gate_a, z_gate_b) = jnp.split(z, splits, axis=-1)

        y_a = conformer_conv(z_ca, z_cb, conv_dw_w[i], conv_dw_b[i],
                             conv_ln_g[i], conv_ln_b[i], conv_w_pw[i])

        q = z_q.reshape(B, T, GLA_HEADS, GLA_HEAD_K) * (GLA_HEAD_K ** -0.5)
        k = z_k.reshape(B, T, GLA_HEADS, GLA_HEAD_K)
        v = z_v.reshape(B, T, GLA_HEADS, GLA_HEAD_V)
        a_logit = (z_lr @ gla_w_alpha[i] + gla_b_alpha[i]).astype(jnp.float32)
        log_a = (jax.nn.log_sigmoid(a_logit) / GLA_TAU).reshape(B, T, GLA_HEADS, GLA_HEAD_K)
        o = gla_chunked(q, k, v, log_a)
        o = o * lax.rsqrt(jnp.mean(o * o, axis=-1, keepdims=True) + EPS)
        o = o * gla_norm[i].astype(jnp.float32).reshape(GLA_HEADS, GLA_HEAD_V)
        o = o.reshape(B, T, GLA_DV).astype(x.dtype) * jax.nn.silu(z_g)
        y_b = o @ gla_w_o[i]

        merged = jax.nn.sigmoid(z_gate_a) * y_a + jax.nn.sigmoid(z_gate_b) * y_b
        h = h + merged @ w_mix_out[i]

        h = h + 0.5 * swiglu(rms_norm(h, ffn2_norm[i]), ffn2_w_in[i], ffn2_w_out[i])

        gate = jax.nn.sigmoid(rms_norm(h, ple_norm[i]) @ ple_w_gate[i])
        h = h + rms_norm(gate * (p[i].astype(h.dtype) @ ple_w_proj[i]), ple_post_norm[i])
    return rms_norm(h, final_norm)
```

```python
import functools

import jax
import jax.numpy as jnp
from jax import lax
from jax.experimental import pallas as pl
from jax.experimental.pallas import tpu as pltpu

D_MODEL = 1024
PLE_DIM = 256
D_FF = 2816
CONV_CH = D_MODEL
CONV_WIDTH = 31
GLA_HEADS = 4
GLA_DK = D_MODEL // 2
GLA_DV = D_MODEL
GLA_HEAD_K = GLA_DK // GLA_HEADS
GLA_HEAD_V = GLA_DV // GLA_HEADS
GLA_GATE_RANK = 16
GLA_TAU = 16.0
GLA_CHUNK = 64
EPS = 1e-6

LANES = 128
CONV_HALO = 32
CUMSUM_GROUP = 256
FFN_TILE = 512
MIX_TILE = 256
VMEM_LIMIT_BYTES = 56 * 1024 * 1024

_OFF_CA, _OFF_CB = 0, CONV_CH
_OFF_Q = 2 * CONV_CH
_OFF_K = _OFF_Q + GLA_DK
_OFF_V = _OFF_K + GLA_DK
_OFF_G = _OFF_V + GLA_DV
_OFF_GA = _OFF_G + GLA_DV
_OFF_GB = _OFF_GA + D_MODEL
_MIX_MAIN = _OFF_GB + D_MODEL

_FF_CHUNKS = tuple((c, min(512, D_FF - c)) for c in range(0, D_FF, 512))

_BF = jnp.bfloat16
_F32 = jnp.float32


def _dot(a, b):
    return jnp.dot(a, b, preferred_element_type=_F32)


def _rms(x, g):
    return x * lax.rsqrt(jnp.mean(x * x, axis=-1, keepdims=True) + EPS) * g


def _swiglu(xn, w_in_ref, w_out_ref):
    acc = None
    for c0, fc in _FF_CHUNKS:
        gate = _dot(xn, w_in_ref[:, c0:c0 + fc])
        up = _dot(xn, w_in_ref[:, D_FF + c0:D_FF + c0 + fc])
        hid = (gate * jax.nn.sigmoid(gate) * up).astype(_BF)
        part = _dot(hid, w_out_ref[c0:c0 + fc, :])
        acc = part if acc is None else acc + part
    return acc


def _ffn1_kernel(x_ref, g_ref, w_in_ref, w_out_ref, o_ref):
    x = x_ref[...]
    xn = _rms(x, g_ref[...]).astype(_BF)
    o_ref[...] = x + 0.5 * _swiglu(xn, w_in_ref, w_out_ref)


def _ffn2_ple_kernel(h_ref, p_ref, g_ref, w_in_ref, w_out_ref, pn_ref, wg_ref, wp_ref,
                     ppn_ref, fn_ref, o_ref):
    h = h_ref[...]
    xn = _rms(h, g_ref[...]).astype(_BF)
    h = h + 0.5 * _swiglu(xn, w_in_ref, w_out_ref)
    gate = jax.nn.sigmoid(_dot(_rms(h, pn_ref[...]).astype(_BF), wg_ref[...]))
    emb = _dot(p_ref[...].astype(_BF), wp_ref[...])
    h = h + _rms(gate * emb, ppn_ref[...])
    o_ref[...] = _rms(h, fn_ref[...])


def _mixer_kernel(h_ref, g_ref, w_ref, wlr_ref, dww_ref, dwb_ref, lng_ref, lnb_ref, wpw_ref,
                  wal_ref, bal_ref, gn_ref, wo_ref, wout_ref, o_ref, ybuf, obuf, st_ref):
    tm = h_ref.shape[1]
    n_chunks = tm // GLA_CHUNK

    @pl.when(pl.program_id(1) == 0)
    def _():
        ybuf[0:CONV_HALO, :] = jnp.zeros((CONV_HALO, CONV_CH), _F32)
        st_ref[...] = jnp.zeros(st_ref.shape, _F32)

    h = h_ref[0]
    u = _rms(h, g_ref[...]).astype(_BF)

    za = _dot(u, w_ref[:, _OFF_CA:_OFF_CA + CONV_CH])
    zb = _dot(u, w_ref[:, _OFF_CB:_OFF_CB + CONV_CH])
    ybuf[CONV_HALO:CONV_HALO + tm, :] = za * jax.nn.sigmoid(zb)
    conv = None
    for j in range(CONV_WIDTH):
        r0 = CONV_HALO - (CONV_WIDTH - 1) + j
        term = ybuf[r0:r0 + tm, :] * dww_ref[j:j + 1, :]
        conv = term if conv is None else conv + term
    conv = conv + dwb_ref[...]
    ybuf[0:CONV_HALO, :] = ybuf[tm:tm + CONV_HALO, :]
    mu = jnp.mean(conv, axis=-1, keepdims=True)
    xc = conv - mu
    yln = xc * lax.rsqrt(jnp.mean(xc * xc, axis=-1, keepdims=True) + EPS) * lng_ref[...] + lnb_ref[...]
    y_a = _dot((yln * jax.nn.sigmoid(yln)).astype(_BF), wpw_ref[...])

    zq = _dot(u, w_ref[:, _OFF_Q:_OFF_Q + GLA_DK]) * (GLA_HEAD_K ** -0.5)
    zk = _dot(u, w_ref[:, _OFF_K:_OFF_K + GLA_DK])
    zv = _dot(u, w_ref[:, _OFF_V:_OFF_V + GLA_DV]).astype(_BF)
    zlr = _dot(u, wlr_ref[...]).astype(_BF)
    a_logit = _dot(zlr, wal_ref[...]) + bal_ref[...]
    log_a = (jnp.minimum(a_logit, 0.0) - jnp.log1p(jnp.exp(-jnp.abs(a_logit)))) * (1.0 / GLA_TAU)

    a_hi = log_a.astype(_BF)
    rem = log_a - a_hi.astype(_F32)
    a_mid = rem.astype(_BF)
    a_lo = (rem - a_mid.astype(_F32)).astype(_BF)
    ri = lax.broadcasted_iota(jnp.int32, (CUMSUM_GROUP, CUMSUM_GROUP), 0)
    ci = lax.broadcasted_iota(jnp.int32, (CUMSUM_GROUP, CUMSUM_GROUP), 1)
    tri = ((ri // GLA_CHUNK == ci // GLA_CHUNK) & (ci <= ri)).astype(_BF)
    cums = []
    for g0 in range(0, tm, CUMSUM_GROUP):
        rows = slice(g0, g0 + CUMSUM_GROUP)
        cums.append(_dot(tri, a_hi[rows]) + _dot(tri, a_mid[rows]) + _dot(tri, a_lo[rows]))
    cum = cums[0] if len(cums) == 1 else jnp.concatenate(cums, axis=0)
    cum_last = jnp.concatenate(
        [jnp.broadcast_to(cum[(c + 1) * GLA_CHUNK - 1:(c + 1) * GLA_CHUNK, :], (GLA_CHUNK, GLA_DK))
         for c in range(n_chunks)], axis=0)
    q_dec = (zq * jnp.exp(cum)).astype(_BF)
    k_inv = (zk * jnp.exp(-cum)).astype(_BF)
    k_dec = (zk * jnp.exp(cum_last - cum)).astype(_BF)
    a_last = jnp.exp(cum_last)

    rc = lax.broadcasted_iota(jnp.int32, (GLA_CHUNK, GLA_CHUNK), 0)
    cc = lax.broadcasted_iota(jnp.int32, (GLA_CHUNK, GLA_CHUNK), 1)
    causal = cc <= rc
    for c in range(n_chunks):
        rows = slice(c * GLA_CHUNK, (c + 1) * GLA_CHUNK)
        for hh in range(GLA_HEADS):
            ks = slice(hh * GLA_HEAD_K, (hh + 1) * GLA_HEAD_K)
            vs = slice(hh * GLA_HEAD_V, (hh + 1) * GLA_HEAD_V)
            qd, ki, kd, vv = q_dec[rows, ks], k_inv[rows, ks], k_dec[rows, ks], zv[rows, vs]
            scores = lax.dot_general(qd, ki, (((1,), (1,)), ((), ())), preferred_element_type=_F32)
            scores = jnp.where(causal, scores, 0.0).astype(_BF)
            st = st_ref[hh]
            o_inter = lax.dot_general(qd, st.astype(_BF), (((1,), (1,)), ((), ())),
                                      preferred_element_type=_F32)
            obuf[rows, vs] = _dot(scores, vv) + o_inter
            upd = lax.dot_general(vv, kd, (((0,), (0,)), ((), ())), preferred_element_type=_F32)
            st_ref[hh] = st * a_last[c * GLA_CHUNK:c * GLA_CHUNK + 1, ks] + upd

    o = obuf[...]
    o_heads = []
    for hh in range(GLA_HEADS):
        oh = o[:, hh * GLA_HEAD_V:(hh + 1) * GLA_HEAD_V]
        o_heads.append(oh * lax.rsqrt(jnp.mean(oh * oh, axis=-1, keepdims=True) + EPS))
    o = jnp.concatenate(o_heads, axis=-1) * gn_ref[...]
    zg = _dot(u, w_ref[:, _OFF_G:_OFF_G + GLA_DV])
    y_b = _dot((o * (zg * jax.nn.sigmoid(zg))).astype(_BF), wo_ref[...])

    zga = _dot(u, w_ref[:, _OFF_GA:_OFF_GA + D_MODEL])
    zgb = _dot(u, w_ref[:, _OFF_GB:_OFF_GB + D_MODEL])
    merged = (jax.nn.sigmoid(zga) * y_a + jax.nn.sigmoid(zgb) * y_b).astype(_BF)
    o_ref[0] = h + _dot(merged, wout_ref[...])


def _resident(shape):
    return pl.BlockSpec(shape, lambda *_: (0,) * len(shape), pipeline_mode=pl.Buffered(1))


def _row(v):
    return v.reshape(1, -1).astype(_F32)


def _ffn1(x, g, w_in, w_out):
    m = x.shape[0]
    return pl.pallas_call(
        _ffn1_kernel,
        grid=(m // FFN_TILE,),
        in_specs=[pl.BlockSpec((FFN_TILE, D_MODEL), lambda i: (i, 0)),
                  _resident((1, D_MODEL)), _resident(w_in.shape), _resident(w_out.shape)],
        out_specs=pl.BlockSpec((FFN_TILE, D_MODEL), lambda i: (i, 0)),
        out_shape=jax.ShapeDtypeStruct((m, D_MODEL), _F32),
        compiler_params=pltpu.CompilerParams(dimension_semantics=("arbitrary",),
                                             vmem_limit_bytes=VMEM_LIMIT_BYTES),
        name="ffn1",
    )(x, g, w_in, w_out)


def _ffn2_ple(h, p, g, w_in, w_out, pn, wg, wp, ppn, fn):
    m = h.shape[0]
    return pl.pallas_call(
        _ffn2_ple_kernel,
        grid=(m // FFN_TILE,),
        in_specs=[pl.BlockSpec((FFN_TILE, D_MODEL), lambda i: (i, 0)),
                  pl.BlockSpec((FFN_TILE, PLE_DIM), lambda i: (i, 0)),
                  _resident((1, D_MODEL)), _resident(w_in.shape), _resident(w_out.shape),
                  _resident((1, D_MODEL)), _resident(wg.shape), _resident(wp.shape),
                  _resident((1, D_MODEL)), _resident((1, D_MODEL))],
        out_specs=pl.BlockSpec((FFN_TILE, D_MODEL), lambda i: (i, 0)),
        out_shape=jax.ShapeDtypeStruct((m, D_MODEL), _F32),
        compiler_params=pltpu.CompilerParams(dimension_semantics=("arbitrary",),
                                             vmem_limit_bytes=VMEM_LIMIT_BYTES),
        name="ffn2_ple",
    )(h, p, g, w_in, w_out, pn, wg, wp, ppn, fn)


def _mixer(h, g, w_main, w_lr, dww, dwb, lng, lnb, wpw, wal, bal, gn, wo, wout):
    b, t, _ = h.shape
    tm = MIX_TILE
    consts = (g, w_main, w_lr, dww, dwb, lng, lnb, wpw, wal, bal, gn, wo, wout)
    return pl.pallas_call(
        _mixer_kernel,
        grid=(b, t // tm),
        in_specs=[pl.BlockSpec((1, tm, D_MODEL), lambda i, j: (i, j, 0))]
                 + [_resident(c.shape) for c in consts],
        out_specs=pl.BlockSpec((1, tm, D_MODEL), lambda i, j: (i, j, 0)),
        out_shape=jax.ShapeDtypeStruct(h.shape, _F32),
        scratch_shapes=[pltpu.VMEM((CONV_HALO + tm, CONV_CH), _F32),
                        pltpu.VMEM((tm, GLA_DV), _F32),
                        pltpu.VMEM((GLA_HEADS, GLA_HEAD_V, GLA_HEAD_K), _F32)],
        compiler_params=pltpu.CompilerParams(dimension_semantics=("arbitrary", "arbitrary"),
                                             vmem_limit_bytes=VMEM_LIMIT_BYTES),
        name="mixer",
    )(h, *consts)


def kernel(x, p, ffn1_norm, ffn1_w_in, ffn1_w_out, mix_norm, w_mix_in, conv_dw_w, conv_dw_b, conv_ln_g, conv_ln_b, conv_w_pw, gla_w_alpha, gla_b_alpha, gla_norm, gla_w_o, w_mix_out, ffn2_norm, ffn2_w_in, ffn2_w_out, ple_norm, ple_w_gate, ple_w_proj, ple_post_norm, final_norm):
    b, t, d = x.shape
    assert ffn1_norm.shape[0] == 1 and d == D_MODEL
    assert t % MIX_TILE == 0 and (b * t) % FFN_TILE == 0 and MIX_TILE % CUMSUM_GROUP == 0
    m = b * t

    h = _ffn1(x.reshape(m, d), _row(ffn1_norm[0]), ffn1_w_in[0].astype(_BF), ffn1_w_out[0].astype(_BF))

    lr0 = _OFF_G + GLA_DV
    w = w_mix_in[0]
    w_main = jnp.concatenate([w[:, :lr0], w[:, lr0 + GLA_GATE_RANK:]], axis=1).astype(_BF)
    w_lr = jnp.pad(w[:, lr0:lr0 + GLA_GATE_RANK], ((0, 0), (0, LANES - GLA_GATE_RANK))).astype(_BF)
    w_alpha = jnp.pad(gla_w_alpha[0], ((0, LANES - GLA_GATE_RANK), (0, 0))).astype(_BF)
    h = _mixer(h.reshape(b, t, d), _row(mix_norm[0]), w_main, w_lr,
               conv_dw_w[0].astype(_F32), _row(conv_dw_b[0]), _row(conv_ln_g[0]), _row(conv_ln_b[0]),
               conv_w_pw[0].astype(_BF), w_alpha, _row(gla_b_alpha[0]), _row(gla_norm[0]),
               gla_w_o[0].astype(_BF), w_mix_out[0].astype(_BF))

    out = _ffn2_ple(h.reshape(m, d), p[0].reshape(m, PLE_DIM), _row(ffn2_norm[0]),
                    ffn2_w_in[0].astype(_BF), ffn2_w_out[0].astype(_BF), _row(ple_norm[0]),
                    ple_w_gate[0].astype(_BF), ple_w_proj[0].astype(_BF), _row(ple_post_norm[0]),
                    _row(final_norm))
    return out.reshape(b, t, d)
```

```python
import functools

import jax
import jax.numpy as jnp
from jax import lax
from jax.experimental import pallas as pl
from jax.experimental.pallas import tpu as pltpu

D_MODEL = 1024
PLE_DIM = 256
D_FF = 2816
CONV_CH = D_MODEL
CONV_WIDTH = 31
GLA_HEADS = 4
GLA_DK = D_MODEL // 2
GLA_DV = D_MODEL
GLA_HEAD_K = GLA_DK // GLA_HEADS
GLA_HEAD_V = GLA_DV // GLA_HEADS
GLA_GATE_RANK = 16
GLA_TAU = 16.0
GLA_CHUNK = 64
EPS = 1e-6

LANES = 128
SUBLANES = 8
CONV_HALO = 32
CONV_ROW_BLOCK = 64
CUMSUM_GROUP = 256
FFN_TILE = 512
MIX_TILE = 256
VMEM_LIMIT_BYTES = 56 * 1024 * 1024

_OFF_CA, _OFF_CB = 0, CONV_CH
_OFF_Q = 2 * CONV_CH
_OFF_K = _OFF_Q + GLA_DK
_OFF_V = _OFF_K + GLA_DK
_OFF_G = _OFF_V + GLA_DV
_OFF_GA = _OFF_G + GLA_DV
_OFF_GB = _OFF_GA + D_MODEL
_MIX_MAIN = _OFF_GB + D_MODEL
_MIX_SEG = 512

_FF_CHUNKS = tuple((c, min(512, D_FF - c)) for c in range(0, D_FF, 512))

_BF = jnp.bfloat16
_F32 = jnp.float32


def _dot(a, b):
    return jnp.dot(a, b, preferred_element_type=_F32)


def _rms(x, g):
    return x * lax.rsqrt(jnp.mean(x * x, axis=-1, keepdims=True) + EPS) * g


def _swiglu(xn, w_in_ref, w_out_ref):
    acc = None
    for c0, fc in _FF_CHUNKS:
        gate = _dot(xn, w_in_ref[:, c0:c0 + fc])
        up = _dot(xn, w_in_ref[:, D_FF + c0:D_FF + c0 + fc])
        hid = (gate * jax.nn.sigmoid(gate) * up).astype(_BF)
        part = _dot(hid, w_out_ref[c0:c0 + fc, :])
        acc = part if acc is None else acc + part
    return acc


def _zero_after(v):
    bits = pltpu.bitcast(v, jnp.uint32)
    return pltpu.bitcast((bits >> 16) >> 16, _F32)[0:1, :]


def _causal_dwconv_lane_tile(ybuf, dww_ref, lane_tile, tm, after=None):
    ls = slice(lane_tile * LANES, (lane_tile + 1) * LANES)
    rb = CONV_ROW_BLOCK
    tiles = rb // SUBLANES
    sub = lax.broadcasted_iota(jnp.int32, (tiles, SUBLANES, LANES), 1)
    taps = [dww_ref[j:j + 1, ls] for j in range(CONV_WIDTH)]
    if after is not None:
        zero = _zero_after(after)
        taps = [w + zero for w in taps]
    blocks = []
    for b0 in range(0, tm, rb):
        acc = None
        for r in range(SUBLANES):
            part = None
            for q in range(-(-(CONV_WIDTH - r) // SUBLANES)):
                r0 = CONV_HALO + b0 - SUBLANES * (q + 1)
                term = ybuf[r0:r0 + rb + SUBLANES, ls] * taps[CONV_WIDTH - 1 - (SUBLANES * q + r)]
                part = term if part is None else part + term
            if r == 0:
                shifted = part[SUBLANES:, :]
            else:
                prev = part[0:rb, :].reshape(tiles, SUBLANES, LANES)
                cur = part[SUBLANES:, :].reshape(tiles, SUBLANES, LANES)
                shifted = pltpu.roll(jnp.where(sub >= SUBLANES - r, prev, cur), r, axis=1).reshape(rb, LANES)
            acc = shifted if acc is None else acc + shifted
        blocks.append(acc)
    return jnp.concatenate(blocks, axis=0)


def _ffn1_kernel(x_ref, g_ref, w_in_ref, w_out_ref, o_ref):
    x = x_ref[...]
    xn = _rms(x, g_ref[...]).astype(_BF)
    o_ref[...] = x + 0.5 * _swiglu(xn, w_in_ref, w_out_ref)


def _ffn2_ple_kernel(h_ref, p_ref, g_ref, w_in_ref, w_out_ref, pn_ref, wg_ref, wp_ref,
                     ppn_ref, fn_ref, o_ref):
    h = h_ref[...]
    xn = _rms(h, g_ref[...]).astype(_BF)
    h = h + 0.5 * _swiglu(xn, w_in_ref, w_out_ref)
    gate = jax.nn.sigmoid(_dot(_rms(h, pn_ref[...]).astype(_BF), wg_ref[...]))
    emb = _dot(p_ref[...].astype(_BF), wp_ref[...])
    h = h + _rms(gate * emb, ppn_ref[...])
    o_ref[...] = _rms(h, fn_ref[...])


def _mixer_kernel(h_ref, g_ref, w_ref, wlr_ref, dww_ref, dwb_ref, lng_ref, lnb_ref, wpw_ref,
                  wal_ref, bal_ref, gn_ref, wo_ref, wout_ref, o_ref, ybuf, obuf, st_ref):
    tm = h_ref.shape[1]
    n_chunks = tm // GLA_CHUNK

    @pl.when(pl.program_id(1) == 0)
    def _():
        ybuf[0:CONV_HALO, :] = jnp.zeros((CONV_HALO, CONV_CH), _F32)
        st_ref[...] = jnp.zeros(st_ref.shape, _F32)

    h = h_ref[0]
    u = _rms(h, g_ref[...]).astype(_BF)

    za = _dot(u, w_ref[:, _OFF_CA:_OFF_CA + CONV_CH])
    zb = _dot(u, w_ref[:, _OFF_CB:_OFF_CB + CONV_CH])
    ybuf[CONV_HALO:CONV_HALO + tm, :] = za * jax.nn.sigmoid(zb)

    z_rest, conv_tiles = [], []
    for i in range((_MIX_MAIN - _OFF_Q) // _MIX_SEG):
        c0 = _OFF_Q + i * _MIX_SEG
        if i < CONV_CH // LANES:
            after = z_rest[-1][tm - SUBLANES:, _MIX_SEG - LANES:] if z_rest else None
            conv_tiles.append(_causal_dwconv_lane_tile(ybuf, dww_ref, i, tm, after))
        z_rest.append(_dot(u, w_ref[:, c0:c0 + _MIX_SEG]))
    z_rest = jnp.concatenate(z_rest, axis=1)

    def seg(off, width):
        return z_rest[:, off - _OFF_Q:off - _OFF_Q + width]

    conv = jnp.concatenate(conv_tiles, axis=1) + dwb_ref[...]
    ybuf[0:CONV_HALO, :] = ybuf[tm:tm + CONV_HALO, :]
    mu = jnp.mean(conv, axis=-1, keepdims=True)
    xc = conv - mu
    yln = xc * lax.rsqrt(jnp.mean(xc * xc, axis=-1, keepdims=True) + EPS) * lng_ref[...] + lnb_ref[...]
    y_a = _dot((yln * jax.nn.sigmoid(yln)).astype(_BF), wpw_ref[...])

    zq = seg(_OFF_Q, GLA_DK) * (GLA_HEAD_K ** -0.5)
    zk = seg(_OFF_K, GLA_DK)
    zv = seg(_OFF_V, GLA_DV).astype(_BF)
    zlr = _dot(u, wlr_ref[...]).astype(_BF)
    a_logit = _dot(zlr, wal_ref[...]) + bal_ref[...]
    log_a = (jnp.minimum(a_logit, 0.0) - jnp.log1p(jnp.exp(-jnp.abs(a_logit)))) * (1.0 / GLA_TAU)

    a_hi = log_a.astype(_BF)
    rem = log_a - a_hi.astype(_F32)
    a_mid = rem.astype(_BF)
    a_lo = (rem - a_mid.astype(_F32)).astype(_BF)
    ri = lax.broadcasted_iota(jnp.int32, (CUMSUM_GROUP, CUMSUM_GROUP), 0)
    ci = lax.broadcasted_iota(jnp.int32, (CUMSUM_GROUP, CUMSUM_GROUP), 1)
    tri = ((ri // GLA_CHUNK == ci // GLA_CHUNK) & (ci <= ri)).astype(_BF)
    cums = []
    for g0 in range(0, tm, CUMSUM_GROUP):
        rows = slice(g0, g0 + CUMSUM_GROUP)
        cums.append(_dot(tri, a_hi[rows]) + _dot(tri, a_mid[rows]) + _dot(tri, a_lo[rows]))
    cum = cums[0] if len(cums) == 1 else jnp.concatenate(cums, axis=0)
    cum_last = jnp.concatenate(
        [jnp.broadcast_to(cum[(c + 1) * GLA_CHUNK - 1:(c + 1) * GLA_CHUNK, :], (GLA_CHUNK, GLA_DK))
         for c in range(n_chunks)], axis=0)
    q_dec = (zq * jnp.exp(cum)).astype(_BF)
    k_inv = (zk * jnp.exp(-cum)).astype(_BF)
    k_dec = (zk * jnp.exp(cum_last - cum)).astype(_BF)
    a_last = jnp.exp(cum_last)

    rc = lax.broadcasted_iota(jnp.int32, (GLA_CHUNK, GLA_CHUNK), 0)
    cc = lax.broadcasted_iota(jnp.int32, (GLA_CHUNK, GLA_CHUNK), 1)
    causal = cc <= rc
    for c in range(n_chunks):
        rows = slice(c * GLA_CHUNK, (c + 1) * GLA_CHUNK)
        for hh in range(GLA_HEADS):
            ks = slice(hh * GLA_HEAD_K, (hh + 1) * GLA_HEAD_K)
            vs = slice(hh * GLA_HEAD_V, (hh + 1) * GLA_HEAD_V)
            qd, ki, kd, vv = q_dec[rows, ks], k_inv[rows, ks], k_dec[rows, ks], zv[rows, vs]
            scores = lax.dot_general(qd, ki, (((1,), (1,)), ((), ())), preferred_element_type=_F32)
            scores = jnp.where(causal, scores, 0.0).astype(_BF)
            st = st_ref[hh]
            o_inter = lax.dot_general(qd, st.astype(_BF), (((1,), (1,)), ((), ())),
                                      preferred_element_type=_F32)
            obuf[rows, vs] = _dot(scores, vv) + o_inter
            upd = lax.dot_general(vv, kd, (((0,), (0,)), ((), ())), preferred_element_type=_F32)
            st_ref[hh] = st * a_last[c * GLA_CHUNK:c * GLA_CHUNK + 1, ks] + upd

    o = obuf[...]
    o_heads = []
    for hh in range(GLA_HEADS):
        oh = o[:, hh * GLA_HEAD_V:(hh + 1) * GLA_HEAD_V]
        o_heads.append(oh * lax.rsqrt(jnp.mean(oh * oh, axis=-1, keepdims=True) + EPS))
    o = jnp.concatenate(o_heads, axis=-1) * gn_ref[...]
    zg = seg(_OFF_G, GLA_DV)
    y_b = _dot((o * (zg * jax.nn.sigmoid(zg))).astype(_BF), wo_ref[...])

    zga = seg(_OFF_GA, D_MODEL)
    zgb = seg(_OFF_GB, D_MODEL)
    merged = (jax.nn.sigmoid(zga) * y_a + jax.nn.sigmoid(zgb) * y_b).astype(_BF)
    o_ref[0] = h + _dot(merged, wout_ref[...])


def _resident(shape):
    return pl.BlockSpec(shape, lambda *_: (0,) * len(shape), pipeline_mode=pl.Buffered(1))


def _row(v):
    return v.reshape(1, -1).astype(_F32)


def _ffn1(x, g, w_in, w_out):
    m = x.shape[0]
    return pl.pallas_call(
        _ffn1_kernel,
        grid=(m // FFN_TILE,),
        in_specs=[pl.BlockSpec((FFN_TILE, D_MODEL), lambda i: (i, 0)),
                  _resident((1, D_MODEL)), _resident(w_in.shape), _resident(w_out.shape)],
        out_specs=pl.BlockSpec((FFN_TILE, D_MODEL), lambda i: (i, 0)),
        out_shape=jax.ShapeDtypeStruct((m, D_MODEL), _F32),
        compiler_params=pltpu.CompilerParams(dimension_semantics=("arbitrary",),
                                             vmem_limit_bytes=VMEM_LIMIT_BYTES),
        name="ffn1",
    )(x, g, w_in, w_out)


def _ffn2_ple(h, p, g, w_in, w_out, pn, wg, wp, ppn, fn):
    m = h.shape[0]
    return pl.pallas_call(
        _ffn2_ple_kernel,
        grid=(m // FFN_TILE,),
        in_specs=[pl.BlockSpec((FFN_TILE, D_MODEL), lambda i: (i, 0)),
                  pl.BlockSpec((FFN_TILE, PLE_DIM), lambda i: (i, 0)),
                  _resident((1, D_MODEL)), _resident(w_in.shape), _resident(w_out.shape),
                  _resident((1, D_MODEL)), _resident(wg.shape), _resident(wp.shape),
                  _resident((1, D_MODEL)), _resident((1, D_MODEL))],
        out_specs=pl.BlockSpec((FFN_TILE, D_MODEL), lambda i: (i, 0)),
        out_shape=jax.ShapeDtypeStruct((m, D_MODEL), _F32),
        compiler_params=pltpu.CompilerParams(dimension_semantics=("arbitrary",),
                                             vmem_limit_bytes=VMEM_LIMIT_BYTES),
        name="ffn2_ple",
    )(h, p, g, w_in, w_out, pn, wg, wp, ppn, fn)


def _mixer(h, g, w_main, w_lr, dww, dwb, lng, lnb, wpw, wal, bal, gn, wo, wout):
    b, t, _ = h.shape
    tm = MIX_TILE
    consts = (g, w_main, w_lr, dww, dwb, lng, lnb, wpw, wal, bal, gn, wo, wout)
    return pl.pallas_call(
        _mixer_kernel,
        grid=(b, t // tm),
        in_specs=[pl.BlockSpec((1, tm, D_MODEL), lambda i, j: (i, j, 0))]
                 + [_resident(c.shape) for c in consts],
        out_specs=pl.BlockSpec((1, tm, D_MODEL), lambda i, j: (i, j, 0)),
        out_shape=jax.ShapeDtypeStruct(h.shape, _F32),
        scratch_shapes=[pltpu.VMEM((CONV_HALO + tm, CONV_CH), _F32),
                        pltpu.VMEM((tm, GLA_DV), _F32),
                        pltpu.VMEM((GLA_HEADS, GLA_HEAD_V, GLA_HEAD_K), _F32)],
        compiler_params=pltpu.CompilerParams(dimension_semantics=("arbitrary", "arbitrary"),
                                             vmem_limit_bytes=VMEM_LIMIT_BYTES),
        name="mixer",
    )(h, *consts)


def kernel(x, p, ffn1_norm, ffn1_w_in, ffn1_w_out, mix_norm, w_mix_in, conv_dw_w, conv_dw_b, conv_ln_g, conv_ln_b, conv_w_pw, gla_w_alpha, gla_b_alpha, gla_norm, gla_w_o, w_mix_out, ffn2_norm, ffn2_w_in, ffn2_w_out, ple_norm, ple_w_gate, ple_w_proj, ple_post_norm, final_norm):
    b, t, d = x.shape
    assert ffn1_norm.shape[0] == 1 and d == D_MODEL
    assert t % MIX_TILE == 0 and (b * t) % FFN_TILE == 0 and MIX_TILE % CUMSUM_GROUP == 0
    m = b * t

    h = _ffn1(x.reshape(m, d), _row(ffn1_norm[0]), ffn1_w_in[0].astype(_BF), ffn1_w_out[0].astype(_BF))

    lr0 = _OFF_G + GLA_DV
    w = w_mix_in[0]
    w_main = jnp.concatenate([w[:, :lr0], w[:, lr0 + GLA_GATE_RANK:]], axis=1).astype(_BF)
    w_lr = jnp.pad(w[:, lr0:lr0 + GLA_GATE_RANK], ((0, 0), (0, LANES - GLA_GATE_RANK))).astype(_BF)
    w_alpha = jnp.pad(gla_w_alpha[0], ((0, LANES - GLA_GATE_RANK), (0, 0))).astype(_BF)
    h = _mixer(h.reshape(b, t, d), _row(mix_norm[0]), w_main, w_lr,
               conv_dw_w[0].astype(_F32), _row(conv_dw_b[0]), _row(conv_ln_g[0]), _row(conv_ln_b[0]),
               conv_w_pw[0].astype(_BF), w_alpha, _row(gla_b_alpha[0]), _row(gla_norm[0]),
               gla_w_o[0].astype(_BF), w_mix_out[0].astype(_BF))

    out = _ffn2_ple(h.reshape(m, d), p[0].reshape(m, PLE_DIM), _row(ffn2_norm[0]),
                    ffn2_w_in[0].astype(_BF), ffn2_w_out[0].astype(_BF), _row(ple_norm[0]),
                    ple_w_gate[0].astype(_BF), ple_w_proj[0].astype(_BF), _row(ple_post_norm[0]),
                    _row(final_norm))
    return out.reshape(b, t, d)
```

```python
import jax
import jax.numpy as jnp
from jax import lax
from jax.experimental import pallas as pl
from jax.experimental.pallas import tpu as pltpu

D_MODEL = 1024
PLE_DIM = 256
D_FF = 2816
CONV_CH = D_MODEL
CONV_WIDTH = 31
GLA_HEADS = 4
GLA_DK = D_MODEL // 2
GLA_DV = D_MODEL
GLA_HEAD_K = GLA_DK // GLA_HEADS
GLA_HEAD_V = GLA_DV // GLA_HEADS
GLA_GATE_RANK = 16
GLA_TAU = 16.0
GLA_CHUNK = 64
EPS = 1e-6

LANES = 128
SUBLANES = 8
CONV_HALO = 32
CONV_ROW_BLOCK = 64
CUMSUM_GROUP = 256
FFN_TILE = 512
MIX_TILE = 512
VMEM_LIMIT_BYTES = 56 * 1024 * 1024

_OFF_Q = 2 * CONV_CH
_OFF_K = _OFF_Q + GLA_DK
_OFF_V = _OFF_K + GLA_DK
_OFF_G = _OFF_V + GLA_DV
_OFF_GA = _OFF_G + GLA_DV
_OFF_GB = _OFF_GA + D_MODEL
_MIX_MAIN = _OFF_GB + D_MODEL
_MIX_SEG = 512
_GLU_BLOCK = 256

_FF_CHUNKS = tuple((c, min(512, D_FF - c)) for c in range(0, D_FF, 512))

_BF = jnp.bfloat16
_F32 = jnp.float32


def _dot(a, b):
    return jnp.dot(a, b, preferred_element_type=_F32)


def _rms(x, g):
    return x * lax.rsqrt(jnp.mean(x * x, axis=-1, keepdims=True) + EPS) * g


def _swiglu(xn, w_in_ref, w_out_ref):
    acc = None
    for c0, fc in _FF_CHUNKS:
        gate = _dot(xn, w_in_ref[:, c0:c0 + fc])
        up = _dot(xn, w_in_ref[:, D_FF + c0:D_FF + c0 + fc])
        hid = (gate * jax.nn.sigmoid(gate) * up).astype(_BF)
        part = _dot(hid, w_out_ref[c0:c0 + fc, :])
        acc = part if acc is None else acc + part
    return acc


def _zero_after(v):
    bits = pltpu.bitcast(v, jnp.uint32)
    return pltpu.bitcast((bits >> 16) >> 16, _F32)[0:1, :]


def _causal_dwconv_lane_tile(ybuf, dww_ref, lane_tile, tm, after=()):
    ls = slice(lane_tile * LANES, (lane_tile + 1) * LANES)
    rb = CONV_ROW_BLOCK
    tiles = rb // SUBLANES
    sub = lax.broadcasted_iota(jnp.int32, (tiles, SUBLANES, LANES), 1)
    taps = [dww_ref[j:j + 1, ls] for j in range(CONV_WIDTH)]
    if after:
        zero = sum(_zero_after(a) for a in after)
        taps = [w + zero for w in taps]
    blocks = []
    for b0 in range(0, tm, rb):
        acc = None
        for r in range(SUBLANES):
            part = None
            for q in range(-(-(CONV_WIDTH - r) // SUBLANES)):
                r0 = CONV_HALO + b0 - SUBLANES * (q + 1)
                term = ybuf[r0:r0 + rb + SUBLANES, ls] * taps[CONV_WIDTH - 1 - (SUBLANES * q + r)]
                part = term if part is None else part + term
            if r == 0:
                shifted = part[SUBLANES:, :]
            else:
                prev = part[0:rb, :].reshape(tiles, SUBLANES, LANES)
                cur = part[SUBLANES:, :].reshape(tiles, SUBLANES, LANES)
                shifted = pltpu.roll(jnp.where(sub >= SUBLANES - r, prev, cur), r, axis=1).reshape(rb, LANES)
            acc = shifted if acc is None else acc + shifted
        blocks.append(acc)
    return jnp.concatenate(blocks, axis=0)


def _ffn1_kernel(x_ref, g_ref, w_in_ref, w_out_ref, o_ref):
    x = x_ref[...]
    xn = _rms(x, g_ref[...]).astype(_BF)
    o_ref[...] = x + 0.5 * _swiglu(xn, w_in_ref, w_out_ref)


def _ffn2_ple_kernel(h_ref, p_ref, g_ref, w_in_ref, w_out_ref, pn_ref, wg_ref, wp_ref,
                     ppn_ref, fn_ref, o_ref):
    h = h_ref[...]
    xn = _rms(h, g_ref[...]).astype(_BF)
    h = h + 0.5 * _swiglu(xn, w_in_ref, w_out_ref)
    gate = jax.nn.sigmoid(_dot(_rms(h, pn_ref[...]).astype(_BF), wg_ref[...]))
    emb = _dot(p_ref[...].astype(_BF), wp_ref[...])
    h = h + _rms(gate * emb, ppn_ref[...])
    o_ref[...] = _rms(h, fn_ref[...])


def _mixer_kernel(h_ref, g_ref, w_ref, wlr_ref, dww_ref, dwb_ref, lng_ref, lnb_ref, wpw_ref,
                  wal_ref, bal_ref, gn_ref, wo_ref, wout_ref, o_ref, ybuf, st_ref):
    tm = h_ref.shape[1]
    n_chunks = tm // GLA_CHUNK

    @pl.when(pl.program_id(1) == 0)
    def _():
        ybuf[0:CONV_HALO, :] = jnp.zeros((CONV_HALO, CONV_CH), _F32)
        st_ref[...] = jnp.zeros(st_ref.shape, _F32)

    h = h_ref[0]
    u = _rms(h, g_ref[...]).astype(_BF)

    def glu_block(i):
        c0 = 2 * _GLU_BLOCK * i
        zab = _dot(u, w_ref[:, c0:c0 + 2 * _GLU_BLOCK])
        ybuf[CONV_HALO:CONV_HALO + tm, _GLU_BLOCK * i:_GLU_BLOCK * (i + 1)] = (
            zab[:, :_GLU_BLOCK] * jax.nn.sigmoid(zab[:, _GLU_BLOCK:]))

    tiles_per_block = _GLU_BLOCK // LANES
    half = _MIX_SEG
    conv_tiles = []

    def proj(off):
        return _dot(u, w_ref[:, off:off + half])

    def conv_tile(after):
        i = len(conv_tiles)
        if i % tiles_per_block == 0:
            glu_block(i // tiles_per_block)
        anchors = [] if after is None else [after[after.shape[0] - SUBLANES:, after.shape[1] - LANES:]]
        conv_tiles.append(_causal_dwconv_lane_tile(ybuf, dww_ref, i, tm, anchors))

    conv_tile(None)
    zq = proj(_OFF_Q)
    conv_tile(zq)
    zk = proj(_OFF_K)
    conv_tile(zk)
    zv_lo = proj(_OFF_V)
    conv_tile(zv_lo)
    zv_hi = proj(_OFF_V + half)
    conv_tile(zv_hi)
    zg_lo = proj(_OFF_G)
    conv_tile(zg_lo)
    zg_hi = proj(_OFF_G + half)
    conv_tile(zg_hi)
    zga_lo = proj(_OFF_GA)
    conv_tile(zga_lo)
    assert len(conv_tiles) == CONV_CH // LANES
    zga = jnp.concatenate([zga_lo, proj(_OFF_GA + half)], axis=1)
    zgb = jnp.concatenate([proj(_OFF_GB), proj(_OFF_GB + half)], axis=1)
    zg = jnp.concatenate([zg_lo, zg_hi], axis=1)
    zv = jnp.concatenate([zv_lo, zv_hi], axis=1).astype(_BF)
    zq = zq * (GLA_HEAD_K ** -0.5)
    zlr = _dot(u, wlr_ref[...]).astype(_BF)
    a_logit = _dot(zlr, wal_ref[...]) + bal_ref[...]
    log_a = (jnp.minimum(a_logit, 0.0) - jnp.log1p(jnp.exp(-jnp.abs(a_logit)))) * (1.0 / GLA_TAU)

    a_hi = log_a.astype(_BF)
    rem = log_a - a_hi.astype(_F32)
    a_mid = rem.astype(_BF)
    a_lo = (rem - a_mid.astype(_F32)).astype(_BF)
    ri = lax.broadcasted_iota(jnp.int32, (CUMSUM_GROUP, CUMSUM_GROUP), 0)
    ci = lax.broadcasted_iota(jnp.int32, (CUMSUM_GROUP, CUMSUM_GROUP), 1)
    tri = ((ri // GLA_CHUNK == ci // GLA_CHUNK) & (ci <= ri)).astype(_BF)
    cums = []
    for g0 in range(0, tm, CUMSUM_GROUP):
        rows = slice(g0, g0 + CUMSUM_GROUP)
        cums.append(_dot(tri, a_hi[rows]) + _dot(tri, a_mid[rows]) + _dot(tri, a_lo[rows]))
    cum = cums[0] if len(cums) == 1 else jnp.concatenate(cums, axis=0)
    cum_last = jnp.concatenate(
        [jnp.broadcast_to(cum[(c + 1) * GLA_CHUNK - 1:(c + 1) * GLA_CHUNK, :], (GLA_CHUNK, GLA_DK))
         for c in range(n_chunks)], axis=0)
    q_dec = (zq * jnp.exp(cum)).astype(_BF)
    k_inv = (zk * jnp.exp(-cum)).astype(_BF)
    k_dec = (zk * jnp.exp(cum_last - cum)).astype(_BF)
    a_last = jnp.exp(cum_last)

    rc = lax.broadcasted_iota(jnp.int32, (GLA_CHUNK, GLA_CHUNK), 0)
    cc = lax.broadcasted_iota(jnp.int32, (GLA_CHUNK, GLA_CHUNK), 1)
    causal = cc <= rc
    scores, incr = {}, {}
    for c in range(n_chunks):
        rows = slice(c * GLA_CHUNK, (c + 1) * GLA_CHUNK)
        for hh in range(GLA_HEADS):
            ks = slice(hh * GLA_HEAD_K, (hh + 1) * GLA_HEAD_K)
            vs = slice(hh * GLA_HEAD_V, (hh + 1) * GLA_HEAD_V)
            sc = lax.dot_general(q_dec[rows, ks], k_inv[rows, ks], (((1,), (1,)), ((), ())),
                                 preferred_element_type=_F32)
            scores[c, hh] = jnp.where(causal, sc, 0.0).astype(_BF)
            incr[c, hh] = lax.dot_general(zv[rows, vs], k_dec[rows, ks], (((0,), (0,)), ((), ())),
                                          preferred_element_type=_F32)
    o_heads = []
    for hh in range(GLA_HEADS):
        ks = slice(hh * GLA_HEAD_K, (hh + 1) * GLA_HEAD_K)
        vs = slice(hh * GLA_HEAD_V, (hh + 1) * GLA_HEAD_V)
        st = st_ref[hh]
        o_chunks = []
        for c in range(n_chunks):
            rows = slice(c * GLA_CHUNK, (c + 1) * GLA_CHUNK)
            o_inter = lax.dot_general(q_dec[rows, ks], st.astype(_BF), (((1,), (1,)), ((), ())),
                                      preferred_element_type=_F32)
            o_chunks.append(_dot(scores[c, hh], zv[rows, vs]) + o_inter)
            st = st * a_last[c * GLA_CHUNK:c * GLA_CHUNK + 1, ks] + incr[c, hh]
        st_ref[hh] = st
        oh = jnp.concatenate(o_chunks, axis=0)
        o_heads.append(oh * lax.rsqrt(jnp.mean(oh * oh, axis=-1, keepdims=True) + EPS))

    conv = jnp.concatenate(conv_tiles, axis=1) + dwb_ref[...]
    ybuf[0:CONV_HALO, :] = ybuf[tm:tm + CONV_HALO, :]
    mu = jnp.mean(conv, axis=-1, keepdims=True)
    xc = conv - mu
    yln = xc * lax.rsqrt(jnp.mean(xc * xc, axis=-1, keepdims=True) + EPS) * lng_ref[...] + lnb_ref[...]
    y_a = _dot((yln * jax.nn.sigmoid(yln)).astype(_BF), wpw_ref[...])

    o = jnp.concatenate(o_heads, axis=-1) * gn_ref[...]
    y_b = _dot((o * (zg * jax.nn.sigmoid(zg))).astype(_BF), wo_ref[...])

    merged = (jax.nn.sigmoid(zga) * y_a + jax.nn.sigmoid(zgb) * y_b).astype(_BF)
    o_ref[0] = h + _dot(merged, wout_ref[...])


def _resident(shape):
    return pl.BlockSpec(shape, lambda *_: (0,) * len(shape), pipeline_mode=pl.Buffered(1))


def _row(v):
    return v.reshape(1, -1).astype(_F32)


def _ffn1(x, g, w_in, w_out):
    m = x.shape[0]
    return pl.pallas_call(
        _ffn1_kernel,
        grid=(m // FFN_TILE,),
        in_specs=[pl.BlockSpec((FFN_TILE, D_MODEL), lambda i: (i, 0)),
                  _resident((1, D_MODEL)), _resident(w_in.shape), _resident(w_out.shape)],
        out_specs=pl.BlockSpec((FFN_TILE, D_MODEL), lambda i: (i, 0)),
        out_shape=jax.ShapeDtypeStruct((m, D_MODEL), _F32),
        compiler_params=pltpu.CompilerParams(dimension_semantics=("arbitrary",),
                                             vmem_limit_bytes=VMEM_LIMIT_BYTES),
        name="ffn1",
    )(x, g, w_in, w_out)


def _ffn2_ple(h, p, g, w_in, w_out, pn, wg, wp, ppn, fn):
    m = h.shape[0]
    return pl.pallas_call(
        _ffn2_ple_kernel,
        grid=(m // FFN_TILE,),
        in_specs=[pl.BlockSpec((FFN_TILE, D_MODEL), lambda i: (i, 0)),
                  pl.BlockSpec((FFN_TILE, PLE_DIM), lambda i: (i, 0)),
                  _resident((1, D_MODEL)), _resident(w_in.shape), _resident(w_out.shape),
                  _resident((1, D_MODEL)), _resident(wg.shape), _resident(wp.shape),
                  _resident((1, D_MODEL)), _resident((1, D_MODEL))],
        out_specs=pl.BlockSpec((FFN_TILE, D_MODEL), lambda i: (i, 0)),
        out_shape=jax.ShapeDtypeStruct((m, D_MODEL), _F32),
        compiler_params=pltpu.CompilerParams(dimension_semantics=("arbitrary",),
                                             vmem_limit_bytes=VMEM_LIMIT_BYTES),
        name="ffn2_ple",
    )(h, p, g, w_in, w_out, pn, wg, wp, ppn, fn)


def _mixer(h, g, w_main, w_lr, dww, dwb, lng, lnb, wpw, wal, bal, gn, wo, wout):
    b, t, _ = h.shape
    tm = MIX_TILE
    consts = (g, w_main, w_lr, dww, dwb, lng, lnb, wpw, wal, bal, gn, wo, wout)
    return pl.pallas_call(
        _mixer_kernel,
        grid=(b, t // tm),
        in_specs=[pl.BlockSpec((1, tm, D_MODEL), lambda i, j: (i, j, 0))]
                 + [_resident(c.shape) for c in consts],
        out_specs=pl.BlockSpec((1, tm, D_MODEL), lambda i, j: (i, j, 0)),
        out_shape=jax.ShapeDtypeStruct(h.shape, _F32),
        scratch_shapes=[pltpu.VMEM((CONV_HALO + tm, CONV_CH), _F32),
                        pltpu.VMEM((GLA_HEADS, GLA_HEAD_V, GLA_HEAD_K), _F32)],
        compiler_params=pltpu.CompilerParams(dimension_semantics=("arbitrary", "arbitrary"),
                                             vmem_limit_bytes=VMEM_LIMIT_BYTES),
        name="mixer",
    )(h, *consts)


def kernel(x, p, ffn1_norm, ffn1_w_in, ffn1_w_out, mix_norm, w_mix_in, conv_dw_w, conv_dw_b, conv_ln_g, conv_ln_b, conv_w_pw, gla_w_alpha, gla_b_alpha, gla_norm, gla_w_o, w_mix_out, ffn2_norm, ffn2_w_in, ffn2_w_out, ple_norm, ple_w_gate, ple_w_proj, ple_post_norm, final_norm):
    b, t, d = x.shape
    assert ffn1_norm.shape[0] == 1 and d == D_MODEL
    assert t % MIX_TILE == 0 and (b * t) % FFN_TILE == 0 and MIX_TILE % CUMSUM_GROUP == 0
    assert _MIX_SEG == GLA_DK and GLA_DV == 2 * _MIX_SEG
    m = b * t

    h = _ffn1(x.reshape(m, d), _row(ffn1_norm[0]), ffn1_w_in[0].astype(_BF), ffn1_w_out[0].astype(_BF))

    lr0 = _OFF_G + GLA_DV
    w = w_mix_in[0]
    w_glu = jnp.stack([w[:, :CONV_CH].reshape(d, -1, _GLU_BLOCK),
                       w[:, CONV_CH:2 * CONV_CH].reshape(d, -1, _GLU_BLOCK)], axis=2).reshape(d, 2 * CONV_CH)
    w_main = jnp.concatenate([w_glu, w[:, 2 * CONV_CH:lr0], w[:, lr0 + GLA_GATE_RANK:]], axis=1).astype(_BF)
    w_lr = jnp.pad(w[:, lr0:lr0 + GLA_GATE_RANK], ((0, 0), (0, LANES - GLA_GATE_RANK))).astype(_BF)
    w_alpha = jnp.pad(gla_w_alpha[0], ((0, LANES - GLA_GATE_RANK), (0, 0))).astype(_BF)
    h = _mixer(h.reshape(b, t, d), _row(mix_norm[0]), w_main, w_lr,
               conv_dw_w[0].astype(_F32), _row(conv_dw_b[0]), _row(conv_ln_g[0]), _row(conv_ln_b[0]),
               conv_w_pw[0].astype(_BF), w_alpha, _row(gla_b_alpha[0]), _row(gla_norm[0]),
               gla_w_o[0].astype(_BF), w_mix_out[0].astype(_BF))

    out = _ffn2_ple(h.reshape(m, d), p[0].reshape(m, PLE_DIM), _row(ffn2_norm[0]),
                    ffn2_w_in[0].astype(_BF), ffn2_w_out[0].astype(_BF), _row(ple_norm[0]),
                    ple_w_gate[0].astype(_BF), ple_w_proj[0].astype(_BF), _row(ple_post_norm[0]),
                    _row(final_norm))
    return out.reshape(b, t, d)
```

```python
import jax
import jax.numpy as jnp
from jax import lax
from jax.experimental import pallas as pl
from jax.experimental.pallas import tpu as pltpu

D_MODEL = 1024
PLE_DIM = 256
D_FF = 2816
CONV_CH = D_MODEL
CONV_WIDTH = 31
GLA_HEADS = 4
GLA_DK = D_MODEL // 2
GLA_DV = D_MODEL
GLA_HEAD_K = GLA_DK // GLA_HEADS
GLA_HEAD_V = GLA_DV // GLA_HEADS
GLA_GATE_RANK = 16
GLA_TAU = 16.0
GLA_CHUNK = 64
EPS = 1e-6

LANES = 128
SUBLANES = 8
CONV_HALO = 32
CONV_ROW_BLOCK = 64
CUMSUM_GROUP = 256
FFN_TILE = 512
MIX_TILE = 512
VMEM_LIMIT_BYTES = 56 * 1024 * 1024

_OFF_CA = 0
_OFF_CB = _OFF_CA + CONV_CH
_OFF_Q = _OFF_CB + CONV_CH
_OFF_K = _OFF_Q + GLA_DK
_OFF_V = _OFF_K + GLA_DK
_OFF_G = _OFF_V + GLA_DV
_OFF_LR = _OFF_G + GLA_DV
_OFF_GATES = _OFF_LR + GLA_GATE_RANK
_MIX_SEG = 512
_GLU_BLOCK = 256

_FF_CHUNKS = tuple((c, min(512, D_FF - c)) for c in range(0, D_FF, 512))

_BF = jnp.bfloat16
_F32 = jnp.float32


def _dot(a, b):
    return jnp.dot(a, b, preferred_element_type=_F32)


def _rms(x, g):
    return x * lax.rsqrt(jnp.mean(x * x, axis=-1, keepdims=True) + EPS) * g


def _swiglu(xn, w_in_ref, w_out_ref):
    acc = None
    for c0, fc in _FF_CHUNKS:
        gate = _dot(xn, w_in_ref[:, c0:c0 + fc])
        up = _dot(xn, w_in_ref[:, D_FF + c0:D_FF + c0 + fc])
        hid = (gate * jax.nn.sigmoid(gate) * up).astype(_BF)
        part = _dot(hid, w_out_ref[c0:c0 + fc, :])
        acc = part if acc is None else acc + part
    return acc


def _zero_after(v):
    bits = pltpu.bitcast(v, jnp.uint32)
    return pltpu.bitcast((bits >> 16) >> 16, _F32)[0:1, :]


def _causal_dwconv_lane_tile(ybuf, dww_ref, lane_tile, tm, after=()):
    ls = slice(lane_tile * LANES, (lane_tile + 1) * LANES)
    rb = CONV_ROW_BLOCK
    tiles = rb // SUBLANES
    sub = lax.broadcasted_iota(jnp.int32, (tiles, SUBLANES, LANES), 1)
    taps = [dww_ref[j:j + 1, ls] for j in range(CONV_WIDTH)]
    if after:
        zero = sum(_zero_after(a) for a in after)
        taps = [w + zero for w in taps]
    blocks = []
    for b0 in range(0, tm, rb):
        acc = None
        for r in range(SUBLANES):
            part = None
            for q in range(-(-(CONV_WIDTH - r) // SUBLANES)):
                r0 = CONV_HALO + b0 - SUBLANES * (q + 1)
                term = ybuf[r0:r0 + rb + SUBLANES, ls] * taps[CONV_WIDTH - 1 - (SUBLANES * q + r)]
                part = term if part is None else part + term
            if r == 0:
                shifted = part[SUBLANES:, :]
            else:
                prev = part[0:rb, :].reshape(tiles, SUBLANES, LANES)
                cur = part[SUBLANES:, :].reshape(tiles, SUBLANES, LANES)
                shifted = pltpu.roll(jnp.where(sub >= SUBLANES - r, prev, cur), r, axis=1).reshape(rb, LANES)
            acc = shifted if acc is None else acc + shifted
        blocks.append(acc)
    return jnp.concatenate(blocks, axis=0)


def _ffn1_kernel(x_ref, g_ref, w_in_ref, w_out_ref, o_ref):
    x = x_ref[...]
    xn = _rms(x, g_ref[...]).astype(_BF)
    o_ref[...] = x + 0.5 * _swiglu(xn, w_in_ref, w_out_ref)


def _ffn2_ple_kernel(h_ref, p_ref, g_ref, w_in_ref, w_out_ref, pn_ref, wg_ref, wp_ref,
                     ppn_ref, fn_ref, o_ref):
    h = h_ref[...]
    xn = _rms(h, g_ref[...]).astype(_BF)
    h = h + 0.5 * _swiglu(xn, w_in_ref, w_out_ref)
    gate = jax.nn.sigmoid(_dot(_rms(h, pn_ref[...]).astype(_BF), wg_ref[...]))
    emb = _dot(p_ref[...].astype(_BF), wp_ref[...])
    h = h + _rms(gate * emb, ppn_ref[...])
    o_ref[...] = _rms(h, fn_ref[...])


def _mixer_kernel(h_ref, g_ref, w_ref, wgate_ref, dww_ref, dwb_ref, lng_ref, lnb_ref, wpw_ref,
                  wal_ref, bal_ref, gn_ref, wo_ref, wout_ref, o_ref, ybuf, st_ref):
    tm = h_ref.shape[1]
    n_chunks = tm // GLA_CHUNK

    @pl.when(pl.program_id(1) == 0)
    def _():
        ybuf[0:CONV_HALO, :] = jnp.zeros((CONV_HALO, CONV_CH), _F32)
        st_ref[...] = jnp.zeros(st_ref.shape, _F32)

    h = h_ref[0]
    u = _rms(h, g_ref[...]).astype(_BF)

    def udot(w):
        return _dot(u, w)

    def glu_block(i):
        c0 = _GLU_BLOCK * i
        za = udot(w_ref[:, _OFF_CA + c0:_OFF_CA + c0 + _GLU_BLOCK])
        zb = udot(w_ref[:, _OFF_CB + c0:_OFF_CB + c0 + _GLU_BLOCK])
        ybuf[CONV_HALO:CONV_HALO + tm, c0:c0 + _GLU_BLOCK] = za * jax.nn.sigmoid(zb)

    tiles_per_block = _GLU_BLOCK // LANES
    half = _MIX_SEG
    conv_tiles = []

    def proj(off):
        return udot(w_ref[:, off:off + half])

    def conv_tile(after):
        i = len(conv_tiles)
        if i % tiles_per_block == 0:
            glu_block(i // tiles_per_block)
        anchors = [] if after is None else [after[after.shape[0] - SUBLANES:, after.shape[1] - LANES:]]
        conv_tiles.append(_causal_dwconv_lane_tile(ybuf, dww_ref, i, tm, anchors))

    conv_tile(None)
    zq = proj(_OFF_Q)
    conv_tile(zq)
    zk = proj(_OFF_K)
    conv_tile(zk)
    zv_lo = proj(_OFF_V)
    conv_tile(zv_lo)
    zv_hi = proj(_OFF_V + half)
    conv_tile(zv_hi)
    zg_lo = proj(_OFF_G)
    conv_tile(zg_lo)
    zg_hi = proj(_OFF_G + half)
    conv_tile(zg_hi)
    zga_lo = udot(wgate_ref[:, 0:half])
    conv_tile(zga_lo)
    assert len(conv_tiles) == CONV_CH // LANES
    zga = jnp.concatenate([zga_lo, udot(wgate_ref[:, half:2 * half])], axis=1)
    zgb = jnp.concatenate([udot(wgate_ref[:, 2 * half:3 * half]), udot(wgate_ref[:, 3 * half:])], axis=1)
    zg = jnp.concatenate([zg_lo, zg_hi], axis=1)
    zv = jnp.concatenate([zv_lo, zv_hi], axis=1).astype(_BF)
    zq = zq * (GLA_HEAD_K ** -0.5)
    zlr = udot(w_ref[:, _OFF_LR:_OFF_LR + LANES]).astype(_BF)
    a_logit = _dot(zlr, wal_ref[...]) + bal_ref[...]
    log_a = (jnp.minimum(a_logit, 0.0) - jnp.log1p(jnp.exp(-jnp.abs(a_logit)))) * (1.0 / GLA_TAU)

    a_hi = log_a.astype(_BF)
    rem = log_a - a_hi.astype(_F32)
    a_mid = rem.astype(_BF)
    a_lo = (rem - a_mid.astype(_F32)).astype(_BF)
    ri = lax.broadcasted_iota(jnp.int32, (CUMSUM_GROUP, CUMSUM_GROUP), 0)
    ci = lax.broadcasted_iota(jnp.int32, (CUMSUM_GROUP, CUMSUM_GROUP), 1)
    tri = ((ri // GLA_CHUNK == ci // GLA_CHUNK) & (ci <= ri)).astype(_BF)
    cums = []
    for g0 in range(0, tm, CUMSUM_GROUP):
        rows = slice(g0, g0 + CUMSUM_GROUP)
        cums.append(_dot(tri, a_hi[rows]) + _dot(tri, a_mid[rows]) + _dot(tri, a_lo[rows]))
    cum = cums[0] if len(cums) == 1 else jnp.concatenate(cums, axis=0)
    cum_last = jnp.concatenate(
        [jnp.broadcast_to(cum[(c + 1) * GLA_CHUNK - 1:(c + 1) * GLA_CHUNK, :], (GLA_CHUNK, GLA_DK))
         for c in range(n_chunks)], axis=0)
    q_dec = (zq * jnp.exp(cum)).astype(_BF)
    k_inv = (zk * jnp.exp(-cum)).astype(_BF)
    k_dec = (zk * jnp.exp(cum_last - cum)).astype(_BF)
    a_last = jnp.exp(cum_last)

    rc = lax.broadcasted_iota(jnp.int32, (GLA_CHUNK, GLA_CHUNK), 0)
    cc = lax.broadcasted_iota(jnp.int32, (GLA_CHUNK, GLA_CHUNK), 1)
    causal = cc <= rc
    scores, incr = {}, {}
    for c in range(n_chunks):
        rows = slice(c * GLA_CHUNK, (c + 1) * GLA_CHUNK)
        for hh in range(GLA_HEADS):
            ks = slice(hh * GLA_HEAD_K, (hh + 1) * GLA_HEAD_K)
            vs = slice(hh * GLA_HEAD_V, (hh + 1) * GLA_HEAD_V)
            sc = lax.dot_general(q_dec[rows, ks], k_inv[rows, ks], (((1,), (1,)), ((), ())),
                                 preferred_element_type=_F32)
            scores[c, hh] = jnp.where(causal, sc, 0.0).astype(_BF)
            incr[c, hh] = lax.dot_general(zv[rows, vs], k_dec[rows, ks], (((0,), (0,)), ((), ())),
                                          preferred_element_type=_F32)
    o_heads = []
    for hh in range(GLA_HEADS):
        ks = slice(hh * GLA_HEAD_K, (hh + 1) * GLA_HEAD_K)
        vs = slice(hh * GLA_HEAD_V, (hh + 1) * GLA_HEAD_V)
        st = st_ref[hh]
        o_chunks = []
        for c in range(n_chunks):
            rows = slice(c * GLA_CHUNK, (c + 1) * GLA_CHUNK)
            o_inter = lax.dot_general(q_dec[rows, ks], st.astype(_BF), (((1,), (1,)), ((), ())),
                                      preferred_element_type=_F32)
            o_chunks.append(_dot(scores[c, hh], zv[rows, vs]) + o_inter)
            st = st * a_last[c * GLA_CHUNK:c * GLA_CHUNK + 1, ks] + incr[c, hh]
        st_ref[hh] = st
        oh = jnp.concatenate(o_chunks, axis=0)
        o_heads.append(oh * lax.rsqrt(jnp.mean(oh * oh, axis=-1, keepdims=True) + EPS))

    conv = jnp.concatenate(conv_tiles, axis=1) + dwb_ref[...]
    ybuf[0:CONV_HALO, :] = ybuf[tm:tm + CONV_HALO, :]
    mu = jnp.mean(conv, axis=-1, keepdims=True)
    xc = conv - mu
    yln = xc * lax.rsqrt(jnp.mean(xc * xc, axis=-1, keepdims=True) + EPS) * lng_ref[...] + lnb_ref[...]
    y_a = _dot((yln * jax.nn.sigmoid(yln)).astype(_BF), wpw_ref[...])

    o = jnp.concatenate(o_heads, axis=-1) * gn_ref[...]
    y_b = _dot((o * (zg * jax.nn.sigmoid(zg))).astype(_BF), wo_ref[...])

    merged = (jax.nn.sigmoid(zga) * y_a + jax.nn.sigmoid(zgb) * y_b).astype(_BF)
    o_ref[0] = h + _dot(merged, wout_ref[...])


def _resident(shape):
    return pl.BlockSpec(shape, lambda *_: (0,) * len(shape), pipeline_mode=pl.Buffered(1))


def _row(v):
    return v.reshape(1, -1).astype(_F32)


def _ffn1(x, g, w_in, w_out):
    m = x.shape[0]
    return pl.pallas_call(
        _ffn1_kernel,
        grid=(m // FFN_TILE,),
        in_specs=[pl.BlockSpec((FFN_TILE, D_MODEL), lambda i: (i, 0)),
                  _resident((1, D_MODEL)), _resident(w_in.shape), _resident(w_out.shape)],
        out_specs=pl.BlockSpec((FFN_TILE, D_MODEL), lambda i: (i, 0)),
        out_shape=jax.ShapeDtypeStruct((m, D_MODEL), _F32),
        compiler_params=pltpu.CompilerParams(dimension_semantics=("arbitrary",),
                                             vmem_limit_bytes=VMEM_LIMIT_BYTES),
        name="ffn1",
    )(x, g, w_in, w_out)


def _ffn2_ple(h, p, g, w_in, w_out, pn, wg, wp, ppn, fn):
    m = h.shape[0]
    return pl.pallas_call(
        _ffn2_ple_kernel,
        grid=(m // FFN_TILE,),
        in_specs=[pl.BlockSpec((FFN_TILE, D_MODEL), lambda i: (i, 0)),
                  pl.BlockSpec((FFN_TILE, PLE_DIM), lambda i: (i, 0)),
                  _resident((1, D_MODEL)), _resident(w_in.shape), _resident(w_out.shape),
                  _resident((1, D_MODEL)), _resident(wg.shape), _resident(wp.shape),
                  _resident((1, D_MODEL)), _resident((1, D_MODEL))],
        out_specs=pl.BlockSpec((FFN_TILE, D_MODEL), lambda i: (i, 0)),
        out_shape=jax.ShapeDtypeStruct((m, D_MODEL), _F32),
        compiler_params=pltpu.CompilerParams(dimension_semantics=("arbitrary",),
                                             vmem_limit_bytes=VMEM_LIMIT_BYTES),
        name="ffn2_ple",
    )(h, p, g, w_in, w_out, pn, wg, wp, ppn, fn)


def _mixer(h, g, w_all, w_gates, dww, dwb, lng, lnb, wpw, wal, bal, gn, wo, wout):
    b, t, _ = h.shape
    tm = MIX_TILE
    consts = (g, w_all, w_gates, dww, dwb, lng, lnb, wpw, wal, bal, gn, wo, wout)
    return pl.pallas_call(
        _mixer_kernel,
        grid=(b, t // tm),
        in_specs=[pl.BlockSpec((1, tm, D_MODEL), lambda i, j: (i, j, 0))]
                 + [_resident(c.shape) for c in consts],
        out_specs=pl.BlockSpec((1, tm, D_MODEL), lambda i, j: (i, j, 0)),
        out_shape=jax.ShapeDtypeStruct(h.shape, _F32),
        scratch_shapes=[pltpu.VMEM((CONV_HALO + tm, CONV_CH), _F32),
                        pltpu.VMEM((GLA_HEADS, GLA_HEAD_V, GLA_HEAD_K), _F32)],
        compiler_params=pltpu.CompilerParams(dimension_semantics=("arbitrary", "arbitrary"),
                                             vmem_limit_bytes=VMEM_LIMIT_BYTES),
        name="mixer",
    )(h, *consts)


def kernel(x, p, ffn1_norm, ffn1_w_in, ffn1_w_out, mix_norm, w_mix_in, conv_dw_w, conv_dw_b, conv_ln_g, conv_ln_b, conv_w_pw, gla_w_alpha, gla_b_alpha, gla_norm, gla_w_o, w_mix_out, ffn2_norm, ffn2_w_in, ffn2_w_out, ple_norm, ple_w_gate, ple_w_proj, ple_post_norm, final_norm):
    b, t, d = x.shape
    assert ffn1_norm.shape[0] == 1 and d == D_MODEL
    assert t % MIX_TILE == 0 and (b * t) % FFN_TILE == 0 and MIX_TILE % CUMSUM_GROUP == 0
    assert _MIX_SEG == GLA_DK and GLA_DV == 2 * _MIX_SEG and p.shape[0] == 1
    assert w_mix_in.shape[-1] == _OFF_GATES + 2 * D_MODEL
    m = b * t

    h = _ffn1(x.reshape(m, d), _row(ffn1_norm[0]), ffn1_w_in[0].astype(_BF), ffn1_w_out[0].astype(_BF))

    w_all = w_mix_in[0].astype(_BF)
    w_gates = w_all[:, _OFF_GATES:]
    w_alpha = jnp.pad(gla_w_alpha[0], ((0, LANES - GLA_GATE_RANK), (0, 0))).astype(_BF)
    h = _mixer(h.reshape(b, t, d), _row(mix_norm[0]), w_all, w_gates,
               conv_dw_w[0].astype(_F32), _row(conv_dw_b[0]), _row(conv_ln_g[0]), _row(conv_ln_b[0]),
               conv_w_pw[0].astype(_BF), w_alpha, _row(gla_b_alpha[0]), _row(gla_norm[0]),
               gla_w_o[0].astype(_BF), w_mix_out[0].astype(_BF))

    out = _ffn2_ple(h.reshape(m, d), p.reshape(m, PLE_DIM), _row(ffn2_norm[0]),
                    ffn2_w_in[0].astype(_BF), ffn2_w_out[0].astype(_BF), _row(ple_norm[0]),
                    ple_w_gate[0].astype(_BF), ple_w_proj[0].astype(_BF), _row(ple_post_norm[0]),
                    _row(final_norm))
    return out.reshape(b, t, d)
```

```python
import jax
import jax.numpy as jnp
from jax import lax
from jax.experimental import pallas as pl
from jax.experimental.pallas import tpu as pltpu

D_MODEL = 1024
PLE_DIM = 256
D_FF = 2816
CONV_CH = D_MODEL
CONV_WIDTH = 31
GLA_HEADS = 4
GLA_DK = D_MODEL // 2
GLA_DV = D_MODEL
GLA_HEAD_K = GLA_DK // GLA_HEADS
GLA_HEAD_V = GLA_DV // GLA_HEADS
GLA_GATE_RANK = 16
GLA_TAU = 16.0
GLA_CHUNK = 64
EPS = 1e-6

LANES = 128
SUBLANES = 8
CONV_HALO = 32
CONV_ROW_BLOCK = 64
CUMSUM_GROUP = 256
FFN_TILE = 512
MIX_TILE = 512
VMEM_LIMIT_BYTES = 56 * 1024 * 1024

_OFF_CA = 0
_OFF_CB = _OFF_CA + CONV_CH
_OFF_Q = _OFF_CB + CONV_CH
_OFF_K = _OFF_Q + GLA_DK
_OFF_V = _OFF_K + GLA_DK
_OFF_G = _OFF_V + GLA_DV
_OFF_LR = _OFF_G + GLA_DV
_OFF_GATES = _OFF_LR + GLA_GATE_RANK
_MIX_SEG = 512
_GLU_BLOCK = 256

_FF_CHUNKS = tuple((c, min(512, D_FF - c)) for c in range(0, D_FF, 512))

_BF = jnp.bfloat16
_F32 = jnp.float32


def _dot(a, b):
    return jnp.dot(a, b, preferred_element_type=_F32)


def _rms(x, g):
    return x * lax.rsqrt(jnp.mean(x * x, axis=-1, keepdims=True) + EPS) * g


def _swiglu(xn, w_in_ref, w_out_ref):
    acc = None
    for c0, fc in _FF_CHUNKS:
        gate = _dot(xn, w_in_ref[:, c0:c0 + fc])
        up = _dot(xn, w_in_ref[:, D_FF + c0:D_FF + c0 + fc])
        hid = (gate * jax.nn.sigmoid(gate) * up).astype(_BF)
        part = _dot(hid, w_out_ref[c0:c0 + fc, :])
        acc = part if acc is None else acc + part
    return acc


def _zero_after(v):
    bits = pltpu.bitcast(v, jnp.uint32)
    return pltpu.bitcast((bits >> 16) >> 16, _F32)[0:1, :]


def _causal_dwconv_lane_tile(ybuf, dww_ref, lane_tile, tm, after=()):
    ls = slice(lane_tile * LANES, (lane_tile + 1) * LANES)
    rb = CONV_ROW_BLOCK
    tiles = rb // SUBLANES
    sub = lax.broadcasted_iota(jnp.int32, (tiles, SUBLANES, LANES), 1)
    taps = [dww_ref[j:j + 1, ls] for j in range(CONV_WIDTH)]
    if after:
        zero = sum(_zero_after(a) for a in after)
        taps = [w + zero for w in taps]
    blocks = []
    for b0 in range(0, tm, rb):
        acc = None
        for r in range(SUBLANES):
            part = None
            for q in range(-(-(CONV_WIDTH - r) // SUBLANES)):
                r0 = CONV_HALO + b0 - SUBLANES * (q + 1)
                term = ybuf[r0:r0 + rb + SUBLANES, ls] * taps[CONV_WIDTH - 1 - (SUBLANES * q + r)]
                part = term if part is None else part + term
            if r == 0:
                shifted = part[SUBLANES:, :]
            else:
                prev = part[0:rb, :].reshape(tiles, SUBLANES, LANES)
                cur = part[SUBLANES:, :].reshape(tiles, SUBLANES, LANES)
                shifted = pltpu.roll(jnp.where(sub >= SUBLANES - r, prev, cur), r, axis=1).reshape(rb, LANES)
            acc = shifted if acc is None else acc + shifted
        blocks.append(acc)
    return jnp.concatenate(blocks, axis=0)


def _ffn1_kernel(x_ref, g_ref, w_in_ref, w_out_ref, o_ref):
    x = x_ref[...]
    xn = _rms(x, g_ref[...]).astype(_BF)
    o_ref[...] = x + 0.5 * _swiglu(xn, w_in_ref, w_out_ref)


def _ffn2_ple_kernel(h_ref, p_ref, g_ref, w_in_ref, w_out_ref, pn_ref, wg_ref, wp_ref,
                     ppn_ref, fn_ref, o_ref):
    h = h_ref[...]
    xn = _rms(h, g_ref[...]).astype(_BF)
    h = h + 0.5 * _swiglu(xn, w_in_ref, w_out_ref)
    gate = jax.nn.sigmoid(_dot(_rms(h, pn_ref[...]).astype(_BF), wg_ref[...]))
    emb = _dot(p_ref[...].astype(_BF), wp_ref[...])
    h = h + _rms(gate * emb, ppn_ref[...])
    o_ref[...] = _rms(h, fn_ref[...])


def _mixer_kernel(h_ref, g_ref, w_ref, wgate_ref, dww_ref, dwb_ref, lng_ref, lnb_ref, wpw_ref,
                  wal_ref, bal_ref, gn_ref, wo_ref, wout_ref, o_ref, ybuf, st_ref):
    tm = h_ref.shape[1]
    n_chunks = tm // GLA_CHUNK

    @pl.when(pl.program_id(1) == 0)
    def _():
        ybuf[0:CONV_HALO, :] = jnp.zeros((CONV_HALO, CONV_CH), _F32)
        st_ref[...] = jnp.zeros(st_ref.shape, _F32)

    h = h_ref[0]
    u = _rms(h, g_ref[...]).astype(_BF)

    def udot(w):
        return _dot(u, w)

    def glu_block(i):
        c0 = _GLU_BLOCK * i
        za = udot(w_ref[:, _OFF_CA + c0:_OFF_CA + c0 + _GLU_BLOCK])
        zb = udot(w_ref[:, _OFF_CB + c0:_OFF_CB + c0 + _GLU_BLOCK])
        ybuf[CONV_HALO:CONV_HALO + tm, c0:c0 + _GLU_BLOCK] = za * jax.nn.sigmoid(zb)

    tiles_per_block = _GLU_BLOCK // LANES
    half = _MIX_SEG
    conv_tiles = []

    def proj(off):
        return udot(w_ref[:, off:off + half])

    def conv_tile(*after):
        i = len(conv_tiles)
        if i % tiles_per_block == 0:
            glu_block(i // tiles_per_block)
        afters = list(after) + conv_tiles[-1:]
        anchors = [a[a.shape[0] - SUBLANES:, a.shape[1] - LANES:] for a in afters]
        conv_tiles.append(_causal_dwconv_lane_tile(ybuf, dww_ref, i, tm, anchors))

    conv_tile()
    zq = proj(_OFF_Q)
    conv_tile(zq)
    zk = proj(_OFF_K)
    conv_tile(zk)
    zv_lo = proj(_OFF_V)
    zv_hi = proj(_OFF_V + half)
    conv_tile(zv_lo, zv_hi)
    zg_lo = proj(_OFF_G)
    zg_hi = proj(_OFF_G + half)
    conv_tile(zg_lo, zg_hi)
    zga_lo = udot(wgate_ref[:, 0:half])
    zga_hi = udot(wgate_ref[:, half:2 * half])
    conv_tile(zga_lo, zga_hi)
    zgb_lo = udot(wgate_ref[:, 2 * half:3 * half])
    zgb_hi = udot(wgate_ref[:, 3 * half:])
    conv_tile(zgb_lo, zgb_hi)
    conv_tile()
    assert len(conv_tiles) == CONV_CH // LANES
    zga = jnp.concatenate([zga_lo, zga_hi], axis=1)
    zgb = jnp.concatenate([zgb_lo, zgb_hi], axis=1)
    zg = jnp.concatenate([zg_lo, zg_hi], axis=1)
    zv = jnp.concatenate([zv_lo, zv_hi], axis=1).astype(_BF)

    zq = zq * (GLA_HEAD_K ** -0.5)
    zlr = udot(w_ref[:, _OFF_LR:_OFF_LR + LANES]).astype(_BF)
    a_logit = _dot(zlr, wal_ref[...]) + bal_ref[...]
    log_a = (jnp.minimum(a_logit, 0.0) - jnp.log1p(jnp.exp(-jnp.abs(a_logit)))) * (1.0 / GLA_TAU)

    a_hi = log_a.astype(_BF)
    rem = log_a - a_hi.astype(_F32)
    a_mid = rem.astype(_BF)
    a_lo = (rem - a_mid.astype(_F32)).astype(_BF)
    ri = lax.broadcasted_iota(jnp.int32, (CUMSUM_GROUP, CUMSUM_GROUP), 0)
    ci = lax.broadcasted_iota(jnp.int32, (CUMSUM_GROUP, CUMSUM_GROUP), 1)
    tri = ((ri // GLA_CHUNK == ci // GLA_CHUNK) & (ci <= ri)).astype(_BF)
    cums = []
    for g0 in range(0, tm, CUMSUM_GROUP):
        rows = slice(g0, g0 + CUMSUM_GROUP)
        cums.append(_dot(tri, a_hi[rows]) + _dot(tri, a_mid[rows]) + _dot(tri, a_lo[rows]))
    cum = cums[0] if len(cums) == 1 else jnp.concatenate(cums, axis=0)
    cum_last = jnp.concatenate(
        [jnp.broadcast_to(cum[(c + 1) * GLA_CHUNK - 1:(c + 1) * GLA_CHUNK, :], (GLA_CHUNK, GLA_DK))
         for c in range(n_chunks)], axis=0)
    q_dec = (zq * jnp.exp(cum)).astype(_BF)
    k_inv = (zk * jnp.exp(-cum)).astype(_BF)
    k_dec = (zk * jnp.exp(cum_last - cum)).astype(_BF)
    a_last = jnp.exp(cum_last)

    rc = lax.broadcasted_iota(jnp.int32, (GLA_CHUNK, GLA_CHUNK), 0)
    cc = lax.broadcasted_iota(jnp.int32, (GLA_CHUNK, GLA_CHUNK), 1)
    causal = cc <= rc
    scores, incr = {}, {}
    for c in range(n_chunks):
        rows = slice(c * GLA_CHUNK, (c + 1) * GLA_CHUNK)
        for hh in range(GLA_HEADS):
            ks = slice(hh * GLA_HEAD_K, (hh + 1) * GLA_HEAD_K)
            vs = slice(hh * GLA_HEAD_V, (hh + 1) * GLA_HEAD_V)
            sc = lax.dot_general(q_dec[rows, ks], k_inv[rows, ks], (((1,), (1,)), ((), ())),
                                 preferred_element_type=_F32)
            scores[c, hh] = jnp.where(causal, sc, 0.0).astype(_BF)
            incr[c, hh] = lax.dot_general(zv[rows, vs], k_dec[rows, ks], (((0,), (0,)), ((), ())),
                                          preferred_element_type=_F32)
    o_heads = []
    for hh in range(GLA_HEADS):
        ks = slice(hh * GLA_HEAD_K, (hh + 1) * GLA_HEAD_K)
        vs = slice(hh * GLA_HEAD_V, (hh + 1) * GLA_HEAD_V)
        st = st_ref[hh]
        o_chunks = []
        for c in range(n_chunks):
            rows = slice(c * GLA_CHUNK, (c + 1) * GLA_CHUNK)
            o_inter = lax.dot_general(q_dec[rows, ks], st.astype(_BF), (((1,), (1,)), ((), ())),
                                      preferred_element_type=_F32)
            o_chunks.append(_dot(scores[c, hh], zv[rows, vs]) + o_inter)
            st = st * a_last[c * GLA_CHUNK:c * GLA_CHUNK + 1, ks] + incr[c, hh]
        st_ref[hh] = st
        oh = jnp.concatenate(o_chunks, axis=0)
        o_heads.append(oh * lax.rsqrt(jnp.mean(oh * oh, axis=-1, keepdims=True) + EPS))

    conv = jnp.concatenate(conv_tiles, axis=1) + dwb_ref[...]
    ybuf[0:CONV_HALO, :] = ybuf[tm:tm + CONV_HALO, :]
    mu = jnp.mean(conv, axis=-1, keepdims=True)
    xc = conv - mu
    yln = xc * lax.rsqrt(jnp.mean(xc * xc, axis=-1, keepdims=True) + EPS) * lng_ref[...] + lnb_ref[...]
    y_a = _dot((yln * jax.nn.sigmoid(yln)).astype(_BF), wpw_ref[...])

    o = jnp.concatenate(o_heads, axis=-1) * gn_ref[...]
    y_b = _dot((o * (zg * jax.nn.sigmoid(zg))).astype(_BF), wo_ref[...])

    merged = (jax.nn.sigmoid(zga) * y_a + jax.nn.sigmoid(zgb) * y_b).astype(_BF)
    o_ref[0] = h + _dot(merged, wout_ref[...])


def _resident(shape):
    return pl.BlockSpec(shape, lambda *_: (0,) * len(shape), pipeline_mode=pl.Buffered(1))


def _row(v):
    return v.reshape(1, -1).astype(_F32)


def _ffn1(x, g, w_in, w_out):
    m = x.shape[0]
    return pl.pallas_call(
        _ffn1_kernel,
        grid=(m // FFN_TILE,),
        in_specs=[pl.BlockSpec((FFN_TILE, D_MODEL), lambda i: (i, 0)),
                  _resident((1, D_MODEL)), _resident(w_in.shape), _resident(w_out.shape)],
        out_specs=pl.BlockSpec((FFN_TILE, D_MODEL), lambda i: (i, 0)),
        out_shape=jax.ShapeDtypeStruct((m, D_MODEL), _F32),
        compiler_params=pltpu.CompilerParams(dimension_semantics=("arbitrary",),
                                             vmem_limit_bytes=VMEM_LIMIT_BYTES),
        name="ffn1",
    )(x, g, w_in, w_out)


def _ffn2_ple(h, p, g, w_in, w_out, pn, wg, wp, ppn, fn):
    m = h.shape[0]
    return pl.pallas_call(
        _ffn2_ple_kernel,
        grid=(m // FFN_TILE,),
        in_specs=[pl.BlockSpec((FFN_TILE, D_MODEL), lambda i: (i, 0)),
                  pl.BlockSpec((FFN_TILE, PLE_DIM), lambda i: (i, 0)),
                  _resident((1, D_MODEL)), _resident(w_in.shape), _resident(w_out.shape),
                  _resident((1, D_MODEL)), _resident(wg.shape), _resident(wp.shape),
                  _resident((1, D_MODEL)), _resident((1, D_MODEL))],
        out_specs=pl.BlockSpec((FFN_TILE, D_MODEL), lambda i: (i, 0)),
        out_shape=jax.ShapeDtypeStruct((m, D_MODEL), _F32),
        compiler_params=pltpu.CompilerParams(dimension_semantics=("arbitrary",),
                                             vmem_limit_bytes=VMEM_LIMIT_BYTES),
        name="ffn2_ple",
    )(h, p, g, w_in, w_out, pn, wg, wp, ppn, fn)


def _mixer(h, g, w_all, w_gates, dww, dwb, lng, lnb, wpw, wal, bal, gn, wo, wout):
    b, t, _ = h.shape
    tm = MIX_TILE
    consts = (g, w_all, w_gates, dww, dwb, lng, lnb, wpw, wal, bal, gn, wo, wout)
    return pl.pallas_call(
        _mixer_kernel,
        grid=(b, t // tm),
        in_specs=[pl.BlockSpec((1, tm, D_MODEL), lambda i, j: (i, j, 0))]
                 + [_resident(c.shape) for c in consts],
        out_specs=pl.BlockSpec((1, tm, D_MODEL), lambda i, j: (i, j, 0)),
        out_shape=jax.ShapeDtypeStruct(h.shape, _F32),
        scratch_shapes=[pltpu.VMEM((CONV_HALO + tm, CONV_CH), _F32),
                        pltpu.VMEM((GLA_HEADS, GLA_HEAD_V, GLA_HEAD_K), _F32)],
        compiler_params=pltpu.CompilerParams(dimension_semantics=("arbitrary", "arbitrary"),
                                             vmem_limit_bytes=VMEM_LIMIT_BYTES),
        name="mixer",
    )(h, *consts)


def kernel(x, p, ffn1_norm, ffn1_w_in, ffn1_w_out, mix_norm, w_mix_in, conv_dw_w, conv_dw_b, conv_ln_g, conv_ln_b, conv_w_pw, gla_w_alpha, gla_b_alpha, gla_norm, gla_w_o, w_mix_out, ffn2_norm, ffn2_w_in, ffn2_w_out, ple_norm, ple_w_gate, ple_w_proj, ple_post_norm, final_norm):
    b, t, d = x.shape
    assert ffn1_norm.shape[0] == 1 and d == D_MODEL
    assert t % MIX_TILE == 0 and (b * t) % FFN_TILE == 0 and MIX_TILE % CUMSUM_GROUP == 0
    assert _MIX_SEG == GLA_DK and GLA_DV == 2 * _MIX_SEG and p.shape[0] == 1
    assert w_mix_in.shape[-1] == _OFF_GATES + 2 * D_MODEL
    m = b * t

    h = _ffn1(x.reshape(m, d), _row(ffn1_norm[0]), ffn1_w_in[0].astype(_BF), ffn1_w_out[0].astype(_BF))

    w_all = w_mix_in[0].astype(_BF)
    w_gates = w_all[:, _OFF_GATES:]
    w_alpha = jnp.pad(gla_w_alpha[0], ((0, LANES - GLA_GATE_RANK), (0, 0))).astype(_BF)
    h = _mixer(h.reshape(b, t, d), _row(mix_norm[0]), w_all, w_gates,
               conv_dw_w[0].astype(_F32), _row(conv_dw_b[0]), _row(conv_ln_g[0]), _row(conv_ln_b[0]),
               conv_w_pw[0].astype(_BF), w_alpha, _row(gla_b_alpha[0]), _row(gla_norm[0]),
               gla_w_o[0].astype(_BF), w_mix_out[0].astype(_BF))

    out = _ffn2_ple(h.reshape(m, d), p.reshape(m, PLE_DIM), _row(ffn2_norm[0]),
                    ffn2_w_in[0].astype(_BF), ffn2_w_out[0].astype(_BF), _row(ple_norm[0]),
                    ple_w_gate[0].astype(_BF), ple_w_proj[0].astype(_BF), _row(ple_post_norm[0]),
                    _row(final_norm))
    return out.reshape(b, t, d)
```
